```python
import jax, jax.numpy as jnp
from jax import lax
import numpy as np

D_MODEL = 1024
BATCH = 16
SEQ = 4096
DEPTH = 1
DEC_BATCH = 8
DEC_SEQ = 64
PAST_LEN = 2048

CHUNK = 64
Q_BLOCK = 128
N_MEM = 256
FOX_HEADS = 8
FOX_HEAD_DIM = 64
FOX_WIDTH = FOX_HEADS * FOX_HEAD_DIM
MLA_HEADS = 8
MLA_NOPE_DIM = 64
MLA_ROPE_DIM = 32
MLA_V_DIM = 64
MLA_WIDTH = MLA_HEADS * MLA_V_DIM
MLA_Q_RANK = 384
MLA_KV_RANK = 256
MIX_WIDTH = FOX_WIDTH + MLA_WIDTH
MEM_HEADS = 4
MEM_HEAD_DIM = 128
MEM_WIDTH = MEM_HEADS * MEM_HEAD_DIM
D_FF = -(-8 * D_MODEL // (3 * 256)) * 256
ROPE_THETA = 10000.0
RMS_EPS = 1e-6
FORGET_BIAS_INIT = 3.0
FOX_SCALE = FOX_HEAD_DIM ** -0.5
MLA_SCALE = (MLA_NOPE_DIM + MLA_ROPE_DIM) ** -0.5
MEM_SCALE = MEM_HEAD_DIM ** -0.5
IN_SPLITS = (FOX_WIDTH, FOX_WIDTH, FOX_WIDTH, FOX_HEADS, MLA_Q_RANK, MLA_KV_RANK, MLA_ROPE_DIM)
IN_WIDTH = 3 * FOX_WIDTH + FOX_HEADS + MLA_Q_RANK + MLA_KV_RANK + MLA_ROPE_DIM

kernel_name = "hybrid_fox_mla_streaming_step"


def rms_norm(x, g):
    xf = x.astype(jnp.float32)
    y = xf * lax.rsqrt(jnp.mean(xf * xf, axis=-1, keepdims=True) + RMS_EPS)
    return (y * g.astype(jnp.float32)).astype(x.dtype)


def rope_tables(pos):
    half = MLA_ROPE_DIM // 2
    inv = ROPE_THETA ** (-jnp.arange(half, dtype=jnp.float32) / half)
    ang = pos.astype(jnp.float32)[:, None] * inv[None, :]
    return jnp.cos(ang), jnp.sin(ang)


def apply_rope(x, cos, sin):
    xf = x.astype(jnp.float32)
    x1, x2 = jnp.split(xf, 2, axis=-1)
    return jnp.concatenate([x1 * cos - x2 * sin, x2 * cos + x1 * sin], axis=-1).astype(x.dtype)


def attend(q, k, v, q_pos, k_pos, scale, unit, q_bias=None, k_bias=None):
    s = jnp.einsum("bqhd,bkhd->bhqk", q, k).astype(jnp.float32) * scale
    if q_bias is not None:
        s = s + jnp.swapaxes(q_bias, 1, 2)[..., :, None] - jnp.swapaxes(k_bias, 1, 2)[..., None, :]
    visible = (k_pos[None, :] // unit) <= (q_pos[:, None] // unit)
    s = jnp.where(visible, s, -jnp.inf)
    p = jax.nn.softmax(s, axis=-1).astype(v.dtype)
    return jnp.einsum("bhqk,bkhd->bqhd", p, v)


def attend_blocks(q, k, v, q_pos, k_pos, scale, unit, q_bias=None, k_bias=None):
    B, T, H, dk = q.shape
    nb = T // Q_BLOCK
    qb = jnp.swapaxes(q.reshape(B, nb, Q_BLOCK, H, dk), 0, 1)
    pb = q_pos.reshape(nb, Q_BLOCK)
    if q_bias is None:
        out = lax.map(lambda a: attend(a[0], k, v, a[1], k_pos, scale, unit), (qb, pb))
    else:
        bb = jnp.swapaxes(q_bias.reshape(B, nb, Q_BLOCK, H), 0, 1)
        out = lax.map(lambda a: attend(a[0], k, v, a[1], k_pos, scale, unit, a[2], k_bias), (qb, pb, bb))
    return jnp.swapaxes(out, 0, 1).reshape(B, T, H, v.shape[-1])


def project_mixers(h, pos, w_in, b_f, g_cq, w_uq, g_ckv):
    B, T, _ = h.shape
    z = h @ w_in
    offs = np.cumsum(IN_SPLITS)[:-1].tolist()
    q_f, k_f, v_f, f_logit, c_q, c_kv, k_r = jnp.split(z, offs, axis=-1)
    fox_q = q_f.reshape(B, T, FOX_HEADS, FOX_HEAD_DIM)
    fox_k = k_f.reshape(B, T, FOX_HEADS, FOX_HEAD_DIM)
    fox_v = v_f.reshape(B, T, FOX_HEADS, FOX_HEAD_DIM)
    logf = jax.nn.log_sigmoid((f_logit + b_f).astype(jnp.float32))
    cos, sin = rope_tables(pos)
    q_m = (rms_norm(c_q, g_cq) @ w_uq).reshape(B, T, MLA_HEADS, MLA_NOPE_DIM + MLA_ROPE_DIM)
    q_nope, q_rope = jnp.split(q_m, [MLA_NOPE_DIM], axis=-1)
    mla_q = jnp.concatenate([q_nope, apply_rope(q_rope, cos[:, None, :], sin[:, None, :])], axis=-1)
    ckv = rms_norm(c_kv, g_ckv)
    kr = apply_rope(k_r, cos, sin)
    return fox_q, fox_k, fox_v, logf, mla_q, ckv, kr


def mla_keys_values(ckv, kr, w_ukv):
    B, S, _ = ckv.shape
    kv = (ckv @ w_ukv).reshape(B, S, MLA_HEADS, MLA_NOPE_DIM + MLA_V_DIM)
    k_nope, v = jnp.split(kv, [MLA_NOPE_DIM], axis=-1)
    k = jnp.concatenate([k_nope, jnp.broadcast_to(kr[:, :, None, :], (B, S, MLA_HEADS, MLA_ROPE_DIM))], axis=-1)
    return k, v


def memory_kv(mem, g_mem_src, w_mk, w_mv):
    B, M, _ = mem.shape
    m = rms_norm(mem, g_mem_src)
    return ((m @ w_mk).reshape(B, M, MEM_HEADS, MEM_HEAD_DIM),
            (m @ w_mv).reshape(B, M, MEM_HEADS, MEM_HEAD_DIM))


def trunk_layer(x, pos, mem_k, mem_v, past, w):
    (g_mix, w_in, b_f, g_cq, w_uq, g_ckv, w_ukv, g_fox_out, g_mla_out, w_o,
     g_mem, w_mq, w_mo, g_ffn, w_gate, w_up, w_down) = w
    B, T, _ = x.shape
    h = rms_norm(x, g_mix)
    fox_q, fox_k, fox_v, logf, mla_q, ckv, kr = project_mixers(h, pos, w_in, b_f, g_cq, w_uq, g_ckv)
    if past is None:
        cum = jnp.cumsum(logf, axis=1)
        fox_o = attend_blocks(fox_q, fox_k, fox_v, pos, pos, FOX_SCALE, 1, cum, cum)
        mk, mv = mla_keys_values(ckv, kr, w_ukv)
        mla_o = attend_blocks(mla_q, mk, mv, pos, pos, MLA_SCALE, CHUNK)
    else:
        c_fk, c_fv, c_lf, c_ckv, c_kr = past
        P = c_fk.shape[1]
        k_pos = jnp.arange(P + T, dtype=jnp.int32)
        cum = jnp.cumsum(jnp.concatenate([c_lf.astype(jnp.float32), logf], axis=1), axis=1)
        fox_o = attend(fox_q, jnp.concatenate([c_fk, fox_k], axis=1), jnp.concatenate([c_fv, fox_v], axis=1),
                       pos, k_pos, FOX_SCALE, 1, cum[:, P:], cum)
        mk, mv = mla_keys_values(jnp.concatenate([c_ckv, ckv], axis=1), jnp.concatenate([c_kr, kr], axis=1), w_ukv)
        mla_o = attend(mla_q, mk, mv, pos, k_pos, MLA_SCALE, CHUNK)
    mixed = jnp.concatenate([rms_norm(fox_o.reshape(B, T, FOX_WIDTH), g_fox_out),
                             rms_norm(mla_o.reshape(B, T, MLA_WIDTH), g_mla_out)], axis=-1)
    x = x + mixed @ w_o
    q = (rms_norm(x, g_mem) @ w_mq).reshape(B, T, MEM_HEADS, MEM_HEAD_DIM)
    s = jnp.einsum("bqhd,bmhd->bhqm", q, mem_k).astype(jnp.float32) * MEM_SCALE
    p = jax.nn.softmax(s, axis=-1).astype(mem_v.dtype)
    o = jnp.einsum("bhqm,bmhd->bqhd", p, mem_v).reshape(B, T, MEM_WIDTH)
    x = x + o @ w_mo
    hf = rms_norm(x, g_ffn)
    x = x + (jax.nn.silu(hf @ w_gate) * (hf @ w_up)) @ w_down
    return x, (fox_k, fox_v, logf, ckv, kr)


def setup_inputs(seed: int = 0) -> dict:
    key = jax.random.key(seed)
    ks = iter(jax.random.split(key, 40))
    f32 = jnp.float32

    def nrm(shape, scale=1.0):
        return jax.random.normal(next(ks), shape, f32) * scale

    def gain(n):
        return 1.0 + nrm((DEPTH, n), 0.02)

    L = DEPTH
    return {
        "x_prompt": nrm((BATCH, SEQ, D_MODEL)),
        "x_sample": nrm((DEC_BATCH, DEC_SEQ, D_MODEL)),
        "mem_prompt": nrm((BATCH, N_MEM, D_MODEL)),
        "cache_fox_k": nrm((L, DEC_BATCH, PAST_LEN, FOX_HEADS, FOX_HEAD_DIM)),
        "cache_fox_v": nrm((L, DEC_BATCH, PAST_LEN, FOX_HEADS, FOX_HEAD_DIM)),
        "cache_fox_logf": jax.nn.log_sigmoid(FORGET_BIAS_INIT + nrm((L, DEC_BATCH, PAST_LEN, FOX_HEADS))),
        "cache_mla_ckv": nrm((L, DEC_BATCH, PAST_LEN, MLA_KV_RANK)),
        "cache_mla_kr": nrm((L, DEC_BATCH, PAST_LEN, MLA_ROPE_DIM)),
        "cache_mem_k": nrm((L, DEC_BATCH, N_MEM, MEM_HEADS, MEM_HEAD_DIM)),
        "cache_mem_v": nrm((L, DEC_BATCH, N_MEM, MEM_HEADS, MEM_HEAD_DIM)),
        "g_mix": gain(D_MODEL),
        "w_in": nrm((L, D_MODEL, IN_WIDTH), D_MODEL ** -0.5),
        "b_f": FORGET_BIAS_INIT + nrm((L, FOX_HEADS), 0.1),
        "g_cq": gain(MLA_Q_RANK),
        "w_uq": nrm((L, MLA_Q_RANK, MLA_HEADS * (MLA_NOPE_DIM + MLA_ROPE_DIM)), MLA_Q_RANK ** -0.5),
        "g_ckv": gain(MLA_KV_RANK),
        "w_ukv": nrm((L, MLA_KV_RANK, MLA_HEADS * (MLA_NOPE_DIM + MLA_V_DIM)), MLA_KV_RANK ** -0.5),
        "g_fox_out": gain(FOX_WIDTH),
        "g_mla_out": gain(MLA_WIDTH),
        "w_o": nrm((L, MIX_WIDTH, D_MODEL), MIX_WIDTH ** -0.5),
        "g_mem_src": gain(D_MODEL),
        "w_mk": nrm((L, D_MODEL, MEM_WIDTH), D_MODEL ** -0.5),
        "w_mv": nrm((L, D_MODEL, MEM_WIDTH), D_MODEL ** -0.5),
        "g_mem": gain(D_MODEL),
        "w_mq": nrm((L, D_MODEL, MEM_WIDTH), D_MODEL ** -0.5),
        "w_mo": nrm((L, MEM_WIDTH, D_MODEL), MEM_WIDTH ** -0.5),
        "g_ffn": gain(D_MODEL),
        "w_gate": nrm((L, D_MODEL, D_FF), D_MODEL ** -0.5),
        "w_up": nrm((L, D_MODEL, D_FF), D_MODEL ** -0.5),
        "w_down": nrm((L, D_FF, D_MODEL), D_FF ** -0.5),
        "g_final": 1.0 + nrm((D_MODEL,), 0.02),
    }


def reference(x_prompt, x_sample, mem_prompt, cache_fox_k, cache_fox_v, cache_fox_logf,
              cache_mla_ckv, cache_mla_kr, cache_mem_k, cache_mem_v,
              g_mix, w_in, b_f, g_cq, w_uq, g_ckv, w_ukv, g_fox_out, g_mla_out, w_o,
              g_mem_src, w_mk, w_mv, g_mem, w_mq, w_mo, g_ffn, w_gate, w_up, w_down, g_final):
    t_p = x_prompt.shape[1]
    t_s = x_sample.shape[1]
    past_len = cache_fox_k.shape[2]
    pos_p = jnp.arange(t_p, dtype=jnp.int32)
    pos_s = past_len + jnp.arange(t_s, dtype=jnp.int32)
    xp, xs = x_prompt, x_sample
    rows_p, rows_s, mem_p = [], [], []
    for l in range(DEPTH):
        w = (g_mix[l], w_in[l], b_f[l], g_cq[l], w_uq[l], g_ckv[l], w_ukv[l], g_fox_out[l], g_mla_out[l], w_o[l],
             g_mem[l], w_mq[l], w_mo[l], g_ffn[l], w_gate[l], w_up[l], w_down[l])
        mk_p, mv_p = memory_kv(mem_prompt, g_mem_src[l], w_mk[l], w_mv[l])
        xp, rp = trunk_layer(xp, pos_p, mk_p, mv_p, None, w)
        past = (cache_fox_k[l], cache_fox_v[l], cache_fox_logf[l], cache_mla_ckv[l], cache_mla_kr[l])
        xs, rs = trunk_layer(xs, pos_s, cache_mem_k[l], cache_mem_v[l], past, w)
        rows_p.append(rp)
        rows_s.append(rs)
        mem_p.append((mk_p, mv_p))
    y_prompt = rms_norm(xp, g_final)
    y_sample = rms_norm(xs, g_final)
    p_fox_k = jnp.stack([r[0] for r in rows_p])
    p_fox_v = jnp.stack([r[1] for r in rows_p])
    p_fox_logf = jnp.stack([r[2] for r in rows_p])
    p_mla_ckv = jnp.stack([r[3] for r in rows_p])
    p_mla_kr = jnp.stack([r[4] for r in rows_p])
    p_mem_k = jnp.stack([m[0] for m in mem_p])
    p_mem_v = jnp.stack([m[1] for m in mem_p])
    s_fox_k = jnp.stack([r[0] for r in rows_s])
    s_fox_v = jnp.stack([r[1] for r in rows_s])
    s_fox_logf = jnp.stack([r[2] for r in rows_s])
    s_mla_ckv = jnp.stack([r[3] for r in rows_s])
    s_mla_kr = jnp.stack([r[4] for r in rows_s])
    return (y_prompt, y_sample, p_fox_k, p_fox_v, p_fox_logf, p_mla_ckv, p_mla_kr, p_mem_k, p_mem_v,
            s_fox_k, s_fox_v, s_fox_logf, s_mla_ckv, s_mla_kr)
```

```python
import functools

import numpy as np
import jax
import jax.numpy as jnp
from jax import lax
from jax.experimental import pallas as pl
from jax.experimental.pallas import tpu as pltpu

F32 = jnp.float32
BF16 = jnp.bfloat16

CHUNK = 64
FOX_HEADS = 8
FOX_HEAD_DIM = 64
FOX_WIDTH = FOX_HEADS * FOX_HEAD_DIM
MLA_HEADS = 8
MLA_NOPE_DIM = 64
MLA_ROPE_DIM = 32
MLA_V_DIM = 64
MLA_WIDTH = MLA_HEADS * MLA_V_DIM
MLA_Q_RANK = 384
MLA_KV_RANK = 256
MEM_HEADS = 4
MEM_HEAD_DIM = 128
MEM_WIDTH = MEM_HEADS * MEM_HEAD_DIM
ROPE_THETA = 10000.0
RMS_EPS = 1e-6
FOX_SCALE = FOX_HEAD_DIM ** -0.5
MLA_SCALE = (MLA_NOPE_DIM + MLA_ROPE_DIM) ** -0.5
MEM_SCALE = MEM_HEAD_DIM ** -0.5

LANES = 128
SLOTS = FOX_HEADS * LANES
HALF_ROPE = MLA_ROPE_DIM // 2
TAIL_KR = 0
TAIL_F = MLA_ROPE_DIM
BIAS_LANE = FOX_HEAD_DIM
NEG_BIG = -1e30
VMEM_LIMIT = 56 * 1024 * 1024

MAX_ROW_TILE = 512
ATTN_TQ = 512
ATTN_TK = 512
SAMPLE_TK = 256


def _rms(x, g):
    return x * lax.rsqrt(jnp.mean(x * x, axis=-1, keepdims=True) + RMS_EPS) * g


def _split3(x):
    hi = x.astype(BF16)
    r = x - hi.astype(F32)
    mid = r.astype(BF16)
    lo = (r - mid.astype(F32)).astype(BF16)
    return hi, mid, lo


def _lane_iota(shape):
    return lax.broadcasted_iota(jnp.int32, shape, len(shape) - 1)


def _params(*sem):
    return pltpu.CompilerParams(dimension_semantics=sem, vmem_limit_bytes=VMEM_LIMIT)


def _proj_kernel(x_ref, gmix_ref, w1_ref, bf_ref, gcq_ref, wuq_ref, gckv_ref, rope_ref,
                 qf_ref, fk_ref, fv_ref, fvb_ref, ckv_ref, tail_ref, mq_ref):
    fw = FOX_WIDTH
    h = _rms(x_ref[0], gmix_ref[...]).astype(BF16)
    z = jnp.dot(h, w1_ref[...], preferred_element_type=F32)
    qf_ref[0] = (z[:, 0:fw] * FOX_SCALE).astype(BF16)
    fk_ref[0] = z[:, fw:2 * fw]
    v = z[:, 2 * fw:3 * fw]
    fv_ref[0] = v
    fvb_ref[0] = v.astype(BF16)
    o = 3 * fw
    cq = _rms(z[:, o:o + MLA_Q_RANK], gcq_ref[...]).astype(BF16)
    o += MLA_Q_RANK
    ckv_ref[0] = _rms(z[:, o:o + MLA_KV_RANK], gckv_ref[...])
    o += MLA_KV_RANK
    tail = z[:, o:o + LANES]

    rope = rope_ref[...]
    cq_t, saq_t, sbq_t = rope[:, 0:LANES], rope[:, LANES:2 * LANES], rope[:, 2 * LANES:3 * LANES]
    ct_t, sat_t, sbt_t = rope[:, 3 * LANES:4 * LANES], rope[:, 4 * LANES:5 * LANES], rope[:, 5 * LANES:6 * LANES]

    t = tail * ct_t + pltpu.roll(tail, LANES - HALF_ROPE, 1) * sat_t + pltpu.roll(tail, HALF_ROPE, 1) * sbt_t
    f = t + bf_ref[...]
    logf = -(jnp.maximum(-f, 0.0) + jnp.log1p(jnp.exp(-jnp.abs(f))))
    lane = _lane_iota(t.shape)
    tail_ref[0] = jnp.where(lane < TAIL_F, t, jnp.where(lane < TAIL_F + FOX_HEADS, logf, 0.0))

    a = jnp.dot(cq, wuq_ref[...], preferred_element_type=F32)
    for hh in range(MLA_HEADS):
        ah = a[:, hh * LANES:(hh + 1) * LANES]
        qh = ah * cq_t + pltpu.roll(ah, LANES - HALF_ROPE, 1) * saq_t + pltpu.roll(ah, HALF_ROPE, 1) * sbq_t
        mq_ref[0, :, hh * LANES:(hh + 1) * LANES] = qh.astype(BF16)


def _proj(x, gmix, w1, bf_row, gcq, wuq, gckv, rope, tm):
    B, T, D = x.shape
    n1 = w1.shape[1]
    row = lambda w: pl.BlockSpec((1, tm, w), lambda i, b: (b, i, 0))
    full = lambda a: pl.BlockSpec(a.shape, lambda i, b: (0,) * a.ndim)
    out_shape = (
        jax.ShapeDtypeStruct((B, T, FOX_WIDTH), BF16),
        jax.ShapeDtypeStruct((B, T, FOX_WIDTH), F32),
        jax.ShapeDtypeStruct((B, T, FOX_WIDTH), F32),
        jax.ShapeDtypeStruct((B, T, FOX_WIDTH), BF16),
        jax.ShapeDtypeStruct((B, T, MLA_KV_RANK), F32),
        jax.ShapeDtypeStruct((B, T, LANES), F32),
        jax.ShapeDtypeStruct((B, T, SLOTS), BF16),
    )
    return pl.pallas_call(
        _proj_kernel,
        grid=(T // tm, B),
        in_specs=[row(D), full(gmix), full(w1), full(bf_row), full(gcq), full(wuq), full(gckv),
                  pl.BlockSpec((tm, 6 * LANES), lambda i, b: (i, 0))],
        out_specs=(row(FOX_WIDTH), row(FOX_WIDTH), row(FOX_WIDTH), row(FOX_WIDTH),
                   row(MLA_KV_RANK), row(LANES), row(SLOTS)),
        out_shape=out_shape,
        compiler_params=_params("arbitrary", "arbitrary"),
        name="proj",
    )(x, gmix, w1, bf_row, gcq, wuq, gckv, rope)


def _foxprep_kernel(q_ref, k_ref, tail_ref, pq_ref, pk_ref, oneq_ref, onek_ref,
                    qo_ref, ko_ref, carry_ref):
    tm = q_ref.shape[1]

    @pl.when(pl.program_id(1) == 0)
    def _():
        carry_ref[...] = jnp.zeros_like(carry_ref)

    t = tail_ref[0]
    lane = _lane_iota(t.shape)
    lf = jnp.where((lane >= TAIL_F) & (lane < TAIL_F + FOX_HEADS), t, 0.0)
    hi, mid, lo = _split3(lf)
    r = lax.broadcasted_iota(jnp.int32, (tm, tm), 0)
    c = lax.broadcasted_iota(jnp.int32, (tm, tm), 1)
    tri = jnp.where(c <= r, 1.0, 0.0).astype(BF16)
    cum = (jnp.dot(tri, hi, preferred_element_type=F32)
           + jnp.dot(tri, mid, preferred_element_type=F32)
           + jnp.dot(tri, lo, preferred_element_type=F32)
           + carry_ref[...])
    carry_ref[...] = cum[tm - 1:tm, :]

    pieces = jnp.concatenate(_split3(cum), axis=1)
    bq = jnp.dot(pieces, pq_ref[...], preferred_element_type=F32) + oneq_ref[...]
    bk = jnp.dot(pieces, pk_ref[...], preferred_element_type=F32) + onek_ref[...]
    low = lane < FOX_HEAD_DIM
    for j in range(FOX_HEADS // 2):
        qp = q_ref[0, :, j * LANES:(j + 1) * LANES].astype(F32)
        kp = k_ref[0, :, j * LANES:(j + 1) * LANES]
        for e in range(2):
            sl = slice((2 * j + e) * LANES, (2 * j + e + 1) * LANES)
            qs = qp if e == 0 else pltpu.roll(qp, FOX_HEAD_DIM, 1)
            ks = kp if e == 0 else pltpu.roll(kp, FOX_HEAD_DIM, 1)
            qo_ref[0, :, sl] = (jnp.where(low, qs, 0.0) + bq[:, sl]).astype(BF16)
            ko_ref[0, :, sl] = (jnp.where(low, ks, 0.0) + bk[:, sl]).astype(BF16)


def _bias_placement():
    pq = np.zeros((3 * LANES, SLOTS), np.float32)
    pk = np.zeros((3 * LANES, SLOTS), np.float32)
    oneq = np.zeros((1, SLOTS), np.float32)
    onek = np.zeros((1, SLOTS), np.float32)
    for h in range(FOX_HEADS):
        base = h * LANES + BIAS_LANE
        for p in range(3):
            pq[p * LANES + TAIL_F + h, base + p] = 1.0
            onek[0, base + p] = 1.0
            pk[p * LANES + TAIL_F + h, base + 3 + p] = -1.0
            oneq[0, base + 3 + p] = 1.0
    return (jnp.asarray(pq, BF16), jnp.asarray(pk, BF16), jnp.asarray(oneq), jnp.asarray(onek))


def _foxprep(q, k, tail, tm):
    B, S, _ = k.shape
    pq, pk, oneq, onek = _bias_placement()
    row = lambda w: pl.BlockSpec((1, tm, w), lambda b, i: (b, i, 0))
    full = lambda a: pl.BlockSpec(a.shape, lambda b, i: (0,) * a.ndim)
    return pl.pallas_call(
        _foxprep_kernel,
        grid=(B, S // tm),
        in_specs=[row(FOX_WIDTH), row(FOX_WIDTH), row(LANES), full(pq), full(pk), full(oneq), full(onek)],
        out_specs=(row(SLOTS), row(SLOTS)),
        out_shape=(jax.ShapeDtypeStruct((B, S, SLOTS), BF16), jax.ShapeDtypeStruct((B, S, SLOTS), BF16)),
        scratch_shapes=[pltpu.VMEM((1, LANES), F32)],
        compiler_params=_params("arbitrary", "arbitrary"),
        name="foxprep",
    )(q, k, tail, pq, pk, oneq, onek)


def _mlakv_kernel(ckv_ref, tail_ref, wk_ref, wv_ref, ko_ref, vo_ref):
    c = ckv_ref[0].astype(BF16)
    kk = jnp.dot(c, wk_ref[...], preferred_element_type=F32)
    t = tail_ref[0]
    lane = _lane_iota(t.shape)
    in_rope = (lane >= MLA_NOPE_DIM) & (lane < MLA_NOPE_DIM + MLA_ROPE_DIM)
    krp = jnp.where(in_rope, pltpu.roll(t, MLA_NOPE_DIM, 1), 0.0)
    for hh in range(MLA_HEADS):
        sl = slice(hh * LANES, (hh + 1) * LANES)
        ko_ref[0, :, sl] = (kk[:, sl] + krp).astype(BF16)
    vo_ref[0] = jnp.dot(c, wv_ref[...], preferred_element_type=F32).astype(BF16)


def _mlakv(ckv, tail, wk, wv, tm):
    B, S, _ = ckv.shape
    row = lambda w: pl.BlockSpec((1, tm, w), lambda b, i: (b, i, 0))
    full = lambda a: pl.BlockSpec(a.shape, lambda b, i: (0,) * a.ndim)
    return pl.pallas_call(
        _mlakv_kernel,
        grid=(B, S // tm),
        in_specs=[row(MLA_KV_RANK), row(LANES), full(wk), full(wv)],
        out_specs=(row(SLOTS), row(MLA_WIDTH)),
        out_shape=(jax.ShapeDtypeStruct((B, S, SLOTS), BF16), jax.ShapeDtypeStruct((B, S, MLA_WIDTH), BF16)),
        compiler_params=_params("arbitrary", "arbitrary"),
        name="mlakv",
    )(ckv, tail, wk, wv)


def _attn_kernel(q_ref, k_ref, v_ref, o_ref, *, tq, tk, q0, unit):
    tk_shift = tk.bit_length() - 1
    unit_shift = unit.bit_length() - 1
    q_start = q0 + pl.program_id(2) * tq
    n_full = lax.shift_right_logical(q_start + unit, tk_shift)
    n_tiles = lax.shift_right_logical(q_start + tq - 1, tk_shift) + 1
    qs = (q_ref[0, :, 0:LANES], q_ref[0, :, LANES:2 * LANES])

    def step(j, carry, masked):
        ks = pl.multiple_of(j * tk, tk)
        v = v_ref[0, pl.ds(ks, tk), :]
        if masked:
            qpos = q_start + lax.broadcasted_iota(jnp.int32, (tq, tk), 0)
            kpos = ks + lax.broadcasted_iota(jnp.int32, (tq, tk), 1)
            vis = lax.shift_right_logical(kpos, unit_shift) <= lax.shift_right_logical(qpos, unit_shift)
        out = []
        for e in range(2):
            m, l, acc = carry[e]
            k = k_ref[0, pl.ds(ks, tk), e * LANES:(e + 1) * LANES]
            s = lax.dot_general(qs[e], k, (((1,), (1,)), ((), ())), preferred_element_type=F32)
            if masked:
                s = jnp.where(vis, s, NEG_BIG)
            m_new = jnp.maximum(m, jnp.max(s, axis=1, keepdims=True))
            alpha = jnp.exp(m - m_new)
            p = jnp.exp(s - m_new)
            l = alpha * l + jnp.sum(p, axis=1, keepdims=True)
            acc = alpha * acc + jnp.dot(p.astype(BF16), v, preferred_element_type=F32)
            out.append((m_new, l, acc))
        return tuple(out)

    init = tuple((jnp.full((tq, 1), NEG_BIG, F32), jnp.zeros((tq, 1), F32), jnp.zeros((tq, LANES), F32))
                 for _ in range(2))
    carry = lax.fori_loop(0, n_full, functools.partial(step, masked=False), init)
    carry = lax.fori_loop(n_full, n_tiles, functools.partial(step, masked=True), carry)
    (_, l0, a0), (_, l1, a1) = carry
    lane = _lane_iota((tq, LANES))
    o_ref[0] = jnp.where(lane < LANES // 2, a0 / l0, a1 / l1)


def _attention(q, k, v, *, tq, tk, q0, unit, name):
    B, Tq, _ = q.shape
    S = k.shape[1]
    pairs = FOX_HEADS // 2
    return pl.pallas_call(
        functools.partial(_attn_kernel, tq=tq, tk=tk, q0=q0, unit=unit),
        grid=(B, pairs, Tq // tq),
        in_specs=[pl.BlockSpec((1, tq, 2 * LANES), lambda b, p, i: (b, i, p)),
                  pl.BlockSpec((1, S, 2 * LANES), lambda b, p, i: (b, 0, p)),
                  pl.BlockSpec((1, S, LANES), lambda b, p, i: (b, 0, p))],
        out_specs=pl.BlockSpec((1, tq, LANES), lambda b, p, i: (b, i, p)),
        out_shape=jax.ShapeDtypeStruct((B, Tq, pairs * LANES), F32),
        compiler_params=_params("arbitrary", "arbitrary", "arbitrary"),
        name=name,
    )(q, k, v)


def _memkv_kernel(mem_ref, g_ref, wk_ref, wv_ref, k_ref, v_ref):
    m = _rms(mem_ref[0], g_ref[...]).astype(BF16)
    k_ref[0] = jnp.dot(m, wk_ref[...], preferred_element_type=F32)
    v_ref[0] = jnp.dot(m, wv_ref[...], preferred_element_type=F32)


def _memkv(mem, g, wk, wv):
    B, M, D = mem.shape
    full = lambda a: pl.BlockSpec(a.shape, lambda b: (0,) * a.ndim)
    blk = lambda w: pl.BlockSpec((1, M, w), lambda b: (b, 0, 0))
    return pl.pallas_call(
        _memkv_kernel,
        grid=(B,),
        in_specs=[blk(D), full(g), full(wk), full(wv)],
        out_specs=(blk(MEM_WIDTH), blk(MEM_WIDTH)),
        out_shape=(jax.ShapeDtypeStruct((B, M, MEM_WIDTH), F32),) * 2,
        compiler_params=_params("arbitrary"),
        name="memkv",
    )(mem, g, wk, wv)


def _post_kernel(x_ref, fo_ref, mo_ref, mk_ref, mv_ref, gfox_ref, gmla_ref, wo_ref,
                 gmem_ref, wmq_ref, wmo_ref, gffn_ref, wg_ref, wu_ref, wd_ref, gfin_ref, y_ref):
    fo = _rms(fo_ref[0], gfox_ref[...]).astype(BF16)
    mo = _rms(mo_ref[0], gmla_ref[...]).astype(BF16)
    x = (x_ref[0]
         + jnp.dot(fo, wo_ref[0:FOX_WIDTH, :], preferred_element_type=F32)
         + jnp.dot(mo, wo_ref[FOX_WIDTH:FOX_WIDTH + MLA_WIDTH, :], preferred_element_type=F32))

    q = jnp.dot(_rms(x, gmem_ref[...]).astype(BF16), wmq_ref[...], preferred_element_type=F32).astype(BF16)
    heads = []
    for hh in range(MEM_HEADS):
        sl = slice(hh * MEM_HEAD_DIM, (hh + 1) * MEM_HEAD_DIM)
        kh = mk_ref[0, :, sl].astype(BF16)
        vh = mv_ref[0, :, sl].astype(BF16)
        s = lax.dot_general(q[:, sl], kh, (((1,), (1,)), ((), ())), preferred_element_type=F32) * MEM_SCALE
        p = jnp.exp(s - jnp.max(s, axis=1, keepdims=True))
        l = jnp.sum(p, axis=1, keepdims=True)
        heads.append(jnp.dot(p.astype(BF16), vh, preferred_element_type=F32) / l)
    o = jnp.concatenate(heads, axis=1).astype(BF16)
    x = x + jnp.dot(o, wmo_ref[...], preferred_element_type=F32)

    hf = _rms(x, gffn_ref[...]).astype(BF16)
    g = jnp.dot(hf, wg_ref[...], preferred_element_type=F32)
    u = jnp.dot(hf, wu_ref[...], preferred_element_type=F32)
    a = (g * (1.0 / (1.0 + jnp.exp(-g))) * u).astype(BF16)
    x = x + jnp.dot(a, wd_ref[...], preferred_element_type=F32)
    y_ref[0] = _rms(x, gfin_ref[...])


def _post(x, fo, mo, mk, mv, gfox, gmla, wo, gmem, wmq, wmo, gffn, wg, wu, wd, gfin, tm):
    B, T, D = x.shape
    M = mk.shape[1]
    row = lambda w: pl.BlockSpec((1, tm, w), lambda b, i: (b, i, 0))
    memb = pl.BlockSpec((1, M, MEM_WIDTH), lambda b, i: (b, 0, 0))
    full = lambda a: pl.BlockSpec(a.shape, lambda b, i: (0,) * a.ndim, pipeline_mode=pl.Buffered(1))
    return pl.pallas_call(
        _post_kernel,
        grid=(B, T // tm),
        in_specs=[row(D), row(FOX_WIDTH), row(MLA_WIDTH), memb, memb,
                  full(gfox), full(gmla), full(wo), full(gmem), full(wmq), full(wmo),
                  full(gffn), full(wg), full(wu), full(wd), full(gfin)],
        out_specs=row(D),
        out_shape=jax.ShapeDtypeStruct((B, T, D), F32),
        compiler_params=_params("arbitrary", "arbitrary"),
        name="post",
    )(x, fo, mo, mk, mv, gfox, gmla, wo, gmem, wmq, wmo, gffn, wg, wu, wd, gfin)


def _rope_tables(pos):
    inv = ROPE_THETA ** (-jnp.arange(HALF_ROPE, dtype=F32) / HALF_ROPE)
    ang = pos.astype(F32)[:, None] * inv[None, :]
    cos, sin = jnp.cos(ang), jnp.sin(ang)
    T = pos.shape[0]
    z = lambda n: jnp.zeros((T, n), F32)
    one = lambda n: jnp.ones((T, n), F32)
    sc = MLA_SCALE
    rest = LANES - MLA_NOPE_DIM - MLA_ROPE_DIM
    cq = jnp.concatenate([sc * one(MLA_NOPE_DIM), sc * cos, sc * cos, z(rest)], axis=1)
    saq = jnp.concatenate([z(MLA_NOPE_DIM), -sc * sin, z(HALF_ROPE), z(rest)], axis=1)
    sbq = jnp.concatenate([z(MLA_NOPE_DIM), z(HALF_ROPE), sc * sin, z(rest)], axis=1)
    trest = LANES - MLA_ROPE_DIM - FOX_HEADS
    ct = jnp.concatenate([cos, cos, one(FOX_HEADS), z(trest)], axis=1)
    sat = jnp.concatenate([-sin, z(HALF_ROPE), z(FOX_HEADS), z(trest)], axis=1)
    sbt = jnp.concatenate([z(HALF_ROPE), sin, z(FOX_HEADS), z(trest)], axis=1)
    return jnp.concatenate([cq, saq, sbq, ct, sat, sbt], axis=1)


def _prep_weights(w_in, b_f, w_uq, w_ukv):
    D = w_in.shape[0]
    o_f = 3 * FOX_WIDTH
    o_cq = o_f + FOX_HEADS
    o_ckv = o_cq + MLA_Q_RANK
    o_kr = o_ckv + MLA_KV_RANK
    tail_w = jnp.concatenate([w_in[:, o_kr:o_kr + MLA_ROPE_DIM], w_in[:, o_f:o_cq],
                              jnp.zeros((D, LANES - MLA_ROPE_DIM - FOX_HEADS), w_in.dtype)], axis=1)
    w1 = jnp.concatenate([w_in[:, 0:o_f], w_in[:, o_cq:o_kr], tail_w], axis=1).astype(BF16)
    bf_row = jnp.zeros((1, LANES), F32).at[0, TAIL_F:TAIL_F + FOX_HEADS].set(b_f)
    qd = MLA_NOPE_DIM + MLA_ROPE_DIM
    wuq = jnp.pad(w_uq.reshape(MLA_Q_RANK, MLA_HEADS, qd), ((0, 0), (0, 0), (0, LANES - qd)))
    wuq = wuq.reshape(MLA_Q_RANK, SLOTS).astype(BF16)
    kv = w_ukv.reshape(MLA_KV_RANK, MLA_HEADS, MLA_NOPE_DIM + MLA_V_DIM)
    wk = jnp.pad(kv[:, :, :MLA_NOPE_DIM], ((0, 0), (0, 0), (0, LANES - MLA_NOPE_DIM)))
    wk = wk.reshape(MLA_KV_RANK, SLOTS).astype(BF16)
    wv = kv[:, :, MLA_NOPE_DIM:].reshape(MLA_KV_RANK, MLA_WIDTH).astype(BF16)
    return w1, bf_row, wuq, wk, wv


def _row_tile(n):
    return n if n <= MAX_ROW_TILE else MAX_ROW_TILE


def _round_up(n, m):
    return -(-n // m) * m


def _group(x, pos0, past, mem_k, mem_v, wts):
    (gmix, w1, bf_row, gcq, wuq, gckv, wk, wv, gfox, gmla, wo, gmem, wmq, wmo, gffn, wg, wu, wd, gfin) = wts
    B, T, _ = x.shape
    tm = _row_tile(T)
    rope = _rope_tables(pos0 + jnp.arange(T, dtype=jnp.int32))
    qf, fk, fv, fvb, ckv, tail, mq = _proj(x, gmix, w1, bf_row, gcq, wuq, gckv, rope, tm)

    if past is None:
        fq_s, fk_s = _foxprep(qf, fk, tail, tm)
        mk_s, mv_s = _mlakv(ckv, tail, wk, wv, tm)
        fvb_all = fvb
        tq, tk = min(ATTN_TQ, T), min(ATTN_TK, T)
    else:
        c_fk, c_fv, c_lf, c_ckv, c_kr = past
        P = c_fk.shape[1]
        tk = SAMPLE_TK
        S = _round_up(P + T, tk)
        pad = lambda a: jnp.pad(a, ((0, 0), (0, S - a.shape[1]), (0, 0)))
        c_tail = jnp.concatenate([c_kr, c_lf, jnp.zeros((B, P, LANES - MLA_ROPE_DIM - FOX_HEADS), F32)], axis=2)
        tail_all = pad(jnp.concatenate([c_tail, tail], axis=1))
        k_all = pad(jnp.concatenate([c_fk.reshape(B, P, FOX_WIDTH), fk], axis=1))
        q_all = pad(jnp.concatenate([jnp.zeros((B, P, FOX_WIDTH), BF16), qf], axis=1))
        fvb_all = pad(jnp.concatenate([c_fv.reshape(B, P, FOX_WIDTH).astype(BF16), fvb], axis=1))
        ckv_all = pad(jnp.concatenate([c_ckv, ckv], axis=1))
        fq_all, fk_s = _foxprep(q_all, k_all, tail_all, tk)
        fq_s = fq_all[:, P:P + T]
        mk_s, mv_s = _mlakv(ckv_all, tail_all, wk, wv, tk)
        tq = T
    q0 = 0 if past is None else past[0].shape[1]
    fox_o = _attention(fq_s, fk_s, fvb_all, tq=tq, tk=tk, q0=q0, unit=1, name="fox_attn")
    mla_o = _attention(mq, mk_s, mv_s, tq=tq, tk=tk, q0=q0, unit=CHUNK, name="mla_attn")
    y = _post(x, fox_o, mla_o, mem_k, mem_v, gfox, gmla, wo, gmem, wmq, wmo, gffn, wg, wu, wd, gfin, tm)
    return y, (fk, fv, tail, ckv)


def kernel(x_prompt, x_sample, mem_prompt, cache_fox_k, cache_fox_v, cache_fox_logf, cache_mla_ckv, cache_mla_kr, cache_mem_k, cache_mem_v, g_mix, w_in, b_f, g_cq, w_uq, g_ckv, w_ukv, g_fox_out, g_mla_out, w_o, g_mem_src, w_mk, w_mv, g_mem, w_mq, w_mo, g_ffn, w_gate, w_up, w_down, g_final):
    depth = w_in.shape[0]
    assert depth == 1, "one layer per step"
    l = 0
    Bp, Tp, _ = x_prompt.shape
    Bs, Ts, _ = x_sample.shape
    P = cache_fox_k.shape[2]
    row = lambda g: g.reshape(1, -1).astype(F32)
    w1, bf_row, wuq, wk, wv = _prep_weights(w_in[l], b_f[l], w_uq[l], w_ukv[l])
    wts = (row(g_mix[l]), w1, bf_row, row(g_cq[l]), wuq, row(g_ckv[l]), wk, wv,
           row(g_fox_out[l]), row(g_mla_out[l]), w_o[l].astype(BF16), row(g_mem[l]),
           w_mq[l].astype(BF16), w_mo[l].astype(BF16), row(g_ffn[l]),
           w_gate[l].astype(BF16), w_up[l].astype(BF16), w_down[l].astype(BF16), row(g_final))

    mk_p, mv_p = _memkv(mem_prompt, row(g_mem_src[l]), w_mk[l].astype(BF16), w_mv[l].astype(BF16))
    y_p, (fk_p, fv_p, tail_p, ckv_p) = _group(x_prompt, 0, None, mk_p, mv_p, wts)

    M = cache_mem_k.shape[2]
    past = (cache_fox_k[l], cache_fox_v[l], cache_fox_logf[l], cache_mla_ckv[l], cache_mla_kr[l])
    y_s, (fk_s, fv_s, tail_s, ckv_s) = _group(
        x_sample, P, past, cache_mem_k[l].reshape(Bs, M, MEM_WIDTH), cache_mem_v[l].reshape(Bs, M, MEM_WIDTH), wts)

    def rows(fk, fv, tail, ckv, B, T):
        return (fk.reshape(1, B, T, FOX_HEADS, FOX_HEAD_DIM), fv.reshape(1, B, T, FOX_HEADS, FOX_HEAD_DIM),
                tail[None, :, :, TAIL_F:TAIL_F + FOX_HEADS], ckv[None],
                tail[None, :, :, TAIL_KR:TAIL_KR + MLA_ROPE_DIM])

    rp = rows(fk_p, fv_p, tail_p, ckv_p, Bp, Tp)
    rs = rows(fk_s, fv_s, tail_s, ckv_s, Bs, Ts)
    mem_shape = (1, Bp, mem_prompt.shape[1], MEM_HEADS, MEM_HEAD_DIM)
    return (y_p, y_s) + rp + (mk_p.reshape(mem_shape), mv_p.reshape(mem_shape)) + rs
```

```python
import functools

import numpy as np
import jax
import jax.numpy as jnp
from jax import lax
from jax.experimental import pallas as pl
from jax.experimental.pallas import tpu as pltpu

F32 = jnp.float32
BF16 = jnp.bfloat16

CHUNK = 64
FOX_HEADS = 8
FOX_HEAD_DIM = 64
FOX_WIDTH = FOX_HEADS * FOX_HEAD_DIM
MLA_HEADS = 8
MLA_NOPE_DIM = 64
MLA_ROPE_DIM = 32
MLA_V_DIM = 64
MLA_WIDTH = MLA_HEADS * MLA_V_DIM
MLA_Q_RANK = 384
MLA_KV_RANK = 256
MEM_HEADS = 4
MEM_HEAD_DIM = 128
MEM_WIDTH = MEM_HEADS * MEM_HEAD_DIM
ROPE_THETA = 10000.0
RMS_EPS = 1e-6
FOX_SCALE = FOX_HEAD_DIM ** -0.5
MLA_SCALE = (MLA_NOPE_DIM + MLA_ROPE_DIM) ** -0.5
MEM_SCALE = MEM_HEAD_DIM ** -0.5
LOG2E = 1.4426950408889634

LANES = 128
SLOTS = FOX_HEADS * LANES
HALF_ROPE = MLA_ROPE_DIM // 2
TAIL_KR = 0
TAIL_F = MLA_ROPE_DIM
BIAS_LANE = FOX_HEAD_DIM
NEG_BIG = -1e30
VMEM_LIMIT = 56 * 1024 * 1024

MAX_ROW_TILE = 512
ATTN_TQ = 512
ATTN_TK = 512
SAMPLE_TK = 256


def _rms(x, g):
    return x * lax.rsqrt(jnp.mean(x * x, axis=-1, keepdims=True) + RMS_EPS) * g


def _split3(x):
    hi = x.astype(BF16)
    r = x - hi.astype(F32)
    mid = r.astype(BF16)
    lo = (r - mid.astype(F32)).astype(BF16)
    return hi, mid, lo


def _lane_iota(shape):
    return lax.broadcasted_iota(jnp.int32, shape, len(shape) - 1)


def _params(*sem):
    return pltpu.CompilerParams(dimension_semantics=sem, vmem_limit_bytes=VMEM_LIMIT)


def _proj_kernel(x_ref, gmix_ref, w1_ref, bf_ref, gcq_ref, wuq_ref, gckv_ref, rope_ref,
                 qf_ref, fk_ref, fv_ref, fvb_ref, ckv_ref, tail_ref, mq_ref):
    fw = FOX_WIDTH
    h = _rms(x_ref[0], gmix_ref[...]).astype(BF16)
    z = jnp.dot(h, w1_ref[...], preferred_element_type=F32)
    qf_ref[0] = (z[:, 0:fw] * (FOX_SCALE * LOG2E)).astype(BF16)
    fk_ref[0] = z[:, fw:2 * fw]
    v = z[:, 2 * fw:3 * fw]
    fv_ref[0] = v
    fvb_ref[0] = v.astype(BF16)
    o = 3 * fw
    cq = _rms(z[:, o:o + MLA_Q_RANK], gcq_ref[...]).astype(BF16)
    o += MLA_Q_RANK
    ckv_ref[0] = _rms(z[:, o:o + MLA_KV_RANK], gckv_ref[...])
    o += MLA_KV_RANK
    tail = z[:, o:o + LANES]

    rope = rope_ref[...]
    cq_t, saq_t, sbq_t = rope[:, 0:LANES], rope[:, LANES:2 * LANES], rope[:, 2 * LANES:3 * LANES]
    ct_t, sat_t, sbt_t = rope[:, 3 * LANES:4 * LANES], rope[:, 4 * LANES:5 * LANES], rope[:, 5 * LANES:6 * LANES]

    t = tail * ct_t + pltpu.roll(tail, LANES - HALF_ROPE, 1) * sat_t + pltpu.roll(tail, HALF_ROPE, 1) * sbt_t
    f = t + bf_ref[...]
    logf = -(jnp.maximum(-f, 0.0) + jnp.log1p(jnp.exp(-jnp.abs(f))))
    lane = _lane_iota(t.shape)
    tail_ref[0] = jnp.where(lane < TAIL_F, t, jnp.where(lane < TAIL_F + FOX_HEADS, logf, 0.0))

    a = jnp.dot(cq, wuq_ref[...], preferred_element_type=F32)
    for hh in range(MLA_HEADS):
        ah = a[:, hh * LANES:(hh + 1) * LANES]
        qh = ah * cq_t + pltpu.roll(ah, LANES - HALF_ROPE, 1) * saq_t + pltpu.roll(ah, HALF_ROPE, 1) * sbq_t
        mq_ref[0, :, hh * LANES:(hh + 1) * LANES] = qh.astype(BF16)


def _proj(x, gmix, w1, bf_row, gcq, wuq, gckv, rope, tm):
    B, T, D = x.shape
    n1 = w1.shape[1]
    row = lambda w: pl.BlockSpec((1, tm, w), lambda i, b: (b, i, 0))
    full = lambda a: pl.BlockSpec(a.shape, lambda i, b: (0,) * a.ndim)
    out_shape = (
        jax.ShapeDtypeStruct((B, T, FOX_WIDTH), BF16),
        jax.ShapeDtypeStruct((B, T, FOX_WIDTH), F32),
        jax.ShapeDtypeStruct((B, T, FOX_WIDTH), F32),
        jax.ShapeDtypeStruct((B, T, FOX_WIDTH), BF16),
        jax.ShapeDtypeStruct((B, T, MLA_KV_RANK), F32),
        jax.ShapeDtypeStruct((B, T, LANES), F32),
        jax.ShapeDtypeStruct((B, T, SLOTS), BF16),
    )
    return pl.pallas_call(
        _proj_kernel,
        grid=(T // tm, B),
        in_specs=[row(D), full(gmix), full(w1), full(bf_row), full(gcq), full(wuq), full(gckv),
                  pl.BlockSpec((tm, 6 * LANES), lambda i, b: (i, 0))],
        out_specs=(row(FOX_WIDTH), row(FOX_WIDTH), row(FOX_WIDTH), row(FOX_WIDTH),
                   row(MLA_KV_RANK), row(LANES), row(SLOTS)),
        out_shape=out_shape,
        compiler_params=_params("arbitrary", "arbitrary"),
        name="proj",
    )(x, gmix, w1, bf_row, gcq, wuq, gckv, rope)


def _foxprep_kernel(q_ref, k_ref, tail_ref, pq_ref, pk_ref, oneq_ref, onek_ref,
                    qo_ref, ko_ref, carry_ref):
    tm = q_ref.shape[1]

    @pl.when(pl.program_id(1) == 0)
    def _():
        carry_ref[...] = jnp.zeros_like(carry_ref)

    t = tail_ref[0]
    lane = _lane_iota(t.shape)
    lf = jnp.where((lane >= TAIL_F) & (lane < TAIL_F + FOX_HEADS), t, 0.0)
    hi, mid, lo = _split3(lf)
    r = lax.broadcasted_iota(jnp.int32, (tm, tm), 0)
    c = lax.broadcasted_iota(jnp.int32, (tm, tm), 1)
    tri = jnp.where(c <= r, 1.0, 0.0).astype(BF16)
    cum = (jnp.dot(tri, hi, preferred_element_type=F32)
           + jnp.dot(tri, mid, preferred_element_type=F32)
           + jnp.dot(tri, lo, preferred_element_type=F32)
           + carry_ref[...])
    carry_ref[...] = cum[tm - 1:tm, :]

    pieces = jnp.concatenate(_split3(cum * LOG2E), axis=1)
    bq = jnp.dot(pieces, pq_ref[...], preferred_element_type=F32) + oneq_ref[...]
    bk = jnp.dot(pieces, pk_ref[...], preferred_element_type=F32) + onek_ref[...]
    low = lane < FOX_HEAD_DIM
    for j in range(FOX_HEADS // 2):
        qp = q_ref[0, :, j * LANES:(j + 1) * LANES].astype(F32)
        kp = k_ref[0, :, j * LANES:(j + 1) * LANES]
        for e in range(2):
            sl = slice((2 * j + e) * LANES, (2 * j + e + 1) * LANES)
            qs = qp if e == 0 else pltpu.roll(qp, FOX_HEAD_DIM, 1)
            ks = kp if e == 0 else pltpu.roll(kp, FOX_HEAD_DIM, 1)
            qo_ref[0, :, sl] = (jnp.where(low, qs, 0.0) + bq[:, sl]).astype(BF16)
            ko_ref[0, :, sl] = (jnp.where(low, ks, 0.0) + bk[:, sl]).astype(BF16)


def _bias_placement():
    pq = np.zeros((3 * LANES, SLOTS), np.float32)
    pk = np.zeros((3 * LANES, SLOTS), np.float32)
    oneq = np.zeros((1, SLOTS), np.float32)
    onek = np.zeros((1, SLOTS), np.float32)
    for h in range(FOX_HEADS):
        base = h * LANES + BIAS_LANE
        for p in range(3):
            pq[p * LANES + TAIL_F + h, base + p] = 1.0
            onek[0, base + p] = 1.0
            pk[p * LANES + TAIL_F + h, base + 3 + p] = -1.0
            oneq[0, base + 3 + p] = 1.0
    return (jnp.asarray(pq, BF16), jnp.asarray(pk, BF16), jnp.asarray(oneq), jnp.asarray(onek))


def _foxprep(q, k, tail, tm):
    B, S, _ = k.shape
    pq, pk, oneq, onek = _bias_placement()
    row = lambda w: pl.BlockSpec((1, tm, w), lambda b, i: (b, i, 0))
    full = lambda a: pl.BlockSpec(a.shape, lambda b, i: (0,) * a.ndim)
    return pl.pallas_call(
        _foxprep_kernel,
        grid=(B, S // tm),
        in_specs=[row(FOX_WIDTH), row(FOX_WIDTH), row(LANES), full(pq), full(pk), full(oneq), full(onek)],
        out_specs=(row(SLOTS), row(SLOTS)),
        out_shape=(jax.ShapeDtypeStruct((B, S, SLOTS), BF16), jax.ShapeDtypeStruct((B, S, SLOTS), BF16)),
        scratch_shapes=[pltpu.VMEM((1, LANES), F32)],
        compiler_params=_params("arbitrary", "arbitrary"),
        name="foxprep",
    )(q, k, tail, pq, pk, oneq, onek)


def _mlakv_kernel(ckv_ref, tail_ref, wk_ref, wv_ref, ko_ref, vo_ref):
    c = ckv_ref[0].astype(BF16)
    kk = jnp.dot(c, wk_ref[...], preferred_element_type=F32)
    t = tail_ref[0]
    lane = _lane_iota(t.shape)
    in_rope = (lane >= MLA_NOPE_DIM) & (lane < MLA_NOPE_DIM + MLA_ROPE_DIM)
    krp = jnp.where(in_rope, pltpu.roll(t, MLA_NOPE_DIM, 1), 0.0)
    for hh in range(MLA_HEADS):
        sl = slice(hh * LANES, (hh + 1) * LANES)
        ko_ref[0, :, sl] = (kk[:, sl] + krp).astype(BF16)
    vo_ref[0] = jnp.dot(c, wv_ref[...], preferred_element_type=F32).astype(BF16)


def _mlakv(ckv, tail, wk, wv, tm):
    B, S, _ = ckv.shape
    row = lambda w: pl.BlockSpec((1, tm, w), lambda b, i: (b, i, 0))
    full = lambda a: pl.BlockSpec(a.shape, lambda b, i: (0,) * a.ndim)
    return pl.pallas_call(
        _mlakv_kernel,
        grid=(B, S // tm),
        in_specs=[row(MLA_KV_RANK), row(LANES), full(wk), full(wv)],
        out_specs=(row(SLOTS), row(MLA_WIDTH)),
        out_shape=(jax.ShapeDtypeStruct((B, S, SLOTS), BF16), jax.ShapeDtypeStruct((B, S, MLA_WIDTH), BF16)),
        compiler_params=_params("arbitrary", "arbitrary"),
        name="mlakv",
    )(ckv, tail, wk, wv)


def _attn_kernel(q_ref, k_ref, vt_ref, o_ref, p_ref, acc_ref, *, tq, tk, q0, unit):
    half = LANES // 2
    tk_shift = tk.bit_length() - 1
    unit_shift = unit.bit_length() - 1
    q_start = q0 + pl.program_id(2) * tq
    n_full = lax.shift_right_logical(q_start + unit, tk_shift)
    qs = (q_ref[0, :, 0:LANES], q_ref[0, :, LANES:2 * LANES])
    top = lax.broadcasted_iota(jnp.int32, (LANES, tq), 0) < half

    def scores(j, m, l, masked):
        ks = pl.multiple_of(j * tk, tk)
        slot = j & 1
        if masked:
            kpos = ks + lax.broadcasted_iota(jnp.int32, (tk, tq), 0)
            qpos = q_start + lax.broadcasted_iota(jnp.int32, (tk, tq), 1)
            vis = lax.shift_right_logical(kpos, unit_shift) <= lax.shift_right_logical(qpos, unit_shift)
        m_out, l_out, a_out = [], [], []
        for e in range(2):
            k = k_ref[0, pl.ds(ks, tk), e * LANES:(e + 1) * LANES]
            st = lax.dot_general(k, qs[e], (((1,), (1,)), ((), ())), preferred_element_type=F32)
            if masked:
                st = jnp.where(vis, st, NEG_BIG)
            m_new = jnp.maximum(m[e], jnp.max(st, axis=0, keepdims=True))
            alpha = jnp.exp2(m[e] - m_new)
            p = jnp.exp2(st - m_new)
            l_out.append(alpha * l[e] + jnp.sum(p, axis=0, keepdims=True))
            m_out.append(m_new)
            a_out.append(alpha)
            p_ref[slot, e * tk:(e + 1) * tk, :] = p.astype(BF16)
        return tuple(m_out), tuple(l_out), tuple(a_out)

    def values(j, alpha):
        slot = j & 1
        vt = vt_ref[0, 0, jnp.maximum(j, 0)]
        zero = jnp.zeros((half, tk), BF16)
        vs = jnp.concatenate([jnp.concatenate([vt[0:half], zero], axis=0),
                              jnp.concatenate([zero, vt[half:LANES]], axis=0)], axis=1)
        pv = jnp.dot(vs, p_ref[slot], preferred_element_type=F32)
        acc_ref[...] = acc_ref[...] * jnp.where(top, alpha[0], alpha[1]) + pv

    p_ref[1] = jnp.zeros(p_ref.shape[1:], BF16)
    acc_ref[...] = jnp.zeros_like(acc_ref)
    row = lambda v: jnp.full((1, tq), v, F32)
    init = ((row(NEG_BIG),) * 2, (row(0.0),) * 2, (row(1.0),) * 2)

    def body(j, carry):
        m, l, alpha = carry
        values(j - 1, alpha)
        return scores(j, m, l, False)

    m, l, alpha = lax.fori_loop(0, n_full, body, init)
    values(n_full - 1, alpha)
    m, l, alpha = scores(n_full, m, l, True)
    values(n_full, alpha)
    o_t = acc_ref[...] / jnp.where(top, l[0], l[1])
    o_ref[0] = o_t.T


def _attention(q, k, vt, *, tq, tk, q0, unit, name):
    B, Tq, _ = q.shape
    S = k.shape[1]
    pairs = FOX_HEADS // 2
    assert tk % tq == 0 and q0 % tk == 0 and Tq % tq == 0 and S % tk == 0 and tq % LANES == 0
    assert q0 + Tq <= S
    return pl.pallas_call(
        functools.partial(_attn_kernel, tq=tq, tk=tk, q0=q0, unit=unit),
        grid=(B, pairs, Tq // tq),
        in_specs=[pl.BlockSpec((1, tq, 2 * LANES), lambda b, p, i: (b, i, p)),
                  pl.BlockSpec((1, S, 2 * LANES), lambda b, p, i: (b, 0, p)),
                  pl.BlockSpec((1, 1, S // tk, LANES, tk), lambda b, p, i: (b, p, 0, 0, 0))],
        out_specs=pl.BlockSpec((1, tq, LANES), lambda b, p, i: (b, i, p)),
        out_shape=jax.ShapeDtypeStruct((B, Tq, pairs * LANES), F32),
        scratch_shapes=[pltpu.VMEM((2, 2 * tk, tq), BF16), pltpu.VMEM((LANES, tq), F32)],
        compiler_params=_params("arbitrary", "arbitrary", "arbitrary"),
        name=name,
    )(q, k, vt)


def _values_transposed(v, tk):
    B, S, W = v.shape
    return v.reshape(B, S // tk, tk, W // LANES, LANES).transpose(0, 3, 1, 4, 2)


def _memkv_kernel(mem_ref, g_ref, wk_ref, wv_ref, k_ref, v_ref):
    m = _rms(mem_ref[0], g_ref[...]).astype(BF16)
    k_ref[0] = jnp.dot(m, wk_ref[...], preferred_element_type=F32)
    v_ref[0] = jnp.dot(m, wv_ref[...], preferred_element_type=F32)


def _memkv(mem, g, wk, wv):
    B, M, D = mem.shape
    full = lambda a: pl.BlockSpec(a.shape, lambda b: (0,) * a.ndim)
    blk = lambda w: pl.BlockSpec((1, M, w), lambda b: (b, 0, 0))
    return pl.pallas_call(
        _memkv_kernel,
        grid=(B,),
        in_specs=[blk(D), full(g), full(wk), full(wv)],
        out_specs=(blk(MEM_WIDTH), blk(MEM_WIDTH)),
        out_shape=(jax.ShapeDtypeStruct((B, M, MEM_WIDTH), F32),) * 2,
        compiler_params=_params("arbitrary"),
        name="memkv",
    )(mem, g, wk, wv)


def _post_kernel(x_ref, fo_ref, mo_ref, mk_ref, mv_ref, gfox_ref, gmla_ref, wo_ref,
                 gmem_ref, wmq_ref, wmo_ref, gffn_ref, wg_ref, wu_ref, wd_ref, gfin_ref, y_ref):
    fo = _rms(fo_ref[0], gfox_ref[...]).astype(BF16)
    mo = _rms(mo_ref[0], gmla_ref[...]).astype(BF16)
    x = (x_ref[0]
         + jnp.dot(fo, wo_ref[0:FOX_WIDTH, :], preferred_element_type=F32)
         + jnp.dot(mo, wo_ref[FOX_WIDTH:FOX_WIDTH + MLA_WIDTH, :], preferred_element_type=F32))

    q = jnp.dot(_rms(x, gmem_ref[...]).astype(BF16), wmq_ref[...], preferred_element_type=F32).astype(BF16)
    heads = []
    for hh in range(MEM_HEADS):
        sl = slice(hh * MEM_HEAD_DIM, (hh + 1) * MEM_HEAD_DIM)
        kh = mk_ref[0, :, sl].astype(BF16)
        vh = mv_ref[0, :, sl].astype(BF16)
        s = lax.dot_general(q[:, sl], kh, (((1,), (1,)), ((), ())), preferred_element_type=F32) * MEM_SCALE
        p = jnp.exp(s - jnp.max(s, axis=1, keepdims=True))
        l = jnp.sum(p, axis=1, keepdims=True)
        heads.append(jnp.dot(p.astype(BF16), vh, preferred_element_type=F32) / l)
    o = jnp.concatenate(heads, axis=1).astype(BF16)
    x = x + jnp.dot(o, wmo_ref[...], preferred_element_type=F32)

    hf = _rms(x, gffn_ref[...]).astype(BF16)
    g = jnp.dot(hf, wg_ref[...], preferred_element_type=F32)
    u = jnp.dot(hf, wu_ref[...], preferred_element_type=F32)
    a = (g * (1.0 / (1.0 + jnp.exp(-g))) * u).astype(BF16)
    x = x + jnp.dot(a, wd_ref[...], preferred_element_type=F32)
    y_ref[0] = _rms(x, gfin_ref[...])


def _post(x, fo, mo, mk, mv, gfox, gmla, wo, gmem, wmq, wmo, gffn, wg, wu, wd, gfin, tm):
    B, T, D = x.shape
    M = mk.shape[1]
    row = lambda w: pl.BlockSpec((1, tm, w), lambda b, i: (b, i, 0))
    memb = pl.BlockSpec((1, M, MEM_WIDTH), lambda b, i: (b, 0, 0))
    full = lambda a: pl.BlockSpec(a.shape, lambda b, i: (0,) * a.ndim, pipeline_mode=pl.Buffered(1))
    return pl.pallas_call(
        _post_kernel,
        grid=(B, T // tm),
        in_specs=[row(D), row(FOX_WIDTH), row(MLA_WIDTH), memb, memb,
                  full(gfox), full(gmla), full(wo), full(gmem), full(wmq), full(wmo),
                  full(gffn), full(wg), full(wu), full(wd), full(gfin)],
        out_specs=row(D),
        out_shape=jax.ShapeDtypeStruct((B, T, D), F32),
        compiler_params=_params("arbitrary", "arbitrary"),
        name="post",
    )(x, fo, mo, mk, mv, gfox, gmla, wo, gmem, wmq, wmo, gffn, wg, wu, wd, gfin)


def _rope_tables(pos):
    inv = ROPE_THETA ** (-jnp.arange(HALF_ROPE, dtype=F32) / HALF_ROPE)
    ang = pos.astype(F32)[:, None] * inv[None, :]
    cos, sin = jnp.cos(ang), jnp.sin(ang)
    T = pos.shape[0]
    z = lambda n: jnp.zeros((T, n), F32)
    one = lambda n: jnp.ones((T, n), F32)
    sc = MLA_SCALE * LOG2E
    rest = LANES - MLA_NOPE_DIM - MLA_ROPE_DIM
    cq = jnp.concatenate([sc * one(MLA_NOPE_DIM), sc * cos, sc * cos, z(rest)], axis=1)
    saq = jnp.concatenate([z(MLA_NOPE_DIM), -sc * sin, z(HALF_ROPE), z(rest)], axis=1)
    sbq = jnp.concatenate([z(MLA_NOPE_DIM), z(HALF_ROPE), sc * sin, z(rest)], axis=1)
    trest = LANES - MLA_ROPE_DIM - FOX_HEADS
    ct = jnp.concatenate([cos, cos, one(FOX_HEADS), z(trest)], axis=1)
    sat = jnp.concatenate([-sin, z(HALF_ROPE), z(FOX_HEADS), z(trest)], axis=1)
    sbt = jnp.concatenate([z(HALF_ROPE), sin, z(FOX_HEADS), z(trest)], axis=1)
    return jnp.concatenate([cq, saq, sbq, ct, sat, sbt], axis=1)


def _prep_weights(w_in, b_f, w_uq, w_ukv):
    D = w_in.shape[0]
    o_f = 3 * FOX_WIDTH
    o_cq = o_f + FOX_HEADS
    o_ckv = o_cq + MLA_Q_RANK
    o_kr = o_ckv + MLA_KV_RANK
    tail_w = jnp.concatenate([w_in[:, o_kr:o_kr + MLA_ROPE_DIM], w_in[:, o_f:o_cq],
                              jnp.zeros((D, LANES - MLA_ROPE_DIM - FOX_HEADS), w_in.dtype)], axis=1)
    w1 = jnp.concatenate([w_in[:, 0:o_f], w_in[:, o_cq:o_kr], tail_w], axis=1).astype(BF16)
    bf_row = jnp.zeros((1, LANES), F32).at[0, TAIL_F:TAIL_F + FOX_HEADS].set(b_f)
    qd = MLA_NOPE_DIM + MLA_ROPE_DIM
    wuq = jnp.pad(w_uq.reshape(MLA_Q_RANK, MLA_HEADS, qd), ((0, 0), (0, 0), (0, LANES - qd)))
    wuq = wuq.reshape(MLA_Q_RANK, SLOTS).astype(BF16)
    kv = w_ukv.reshape(MLA_KV_RANK, MLA_HEADS, MLA_NOPE_DIM + MLA_V_DIM)
    wk = jnp.pad(kv[:, :, :MLA_NOPE_DIM], ((0, 0), (0, 0), (0, LANES - MLA_NOPE_DIM)))
    wk = wk.reshape(MLA_KV_RANK, SLOTS).astype(BF16)
    wv = kv[:, :, MLA_NOPE_DIM:].reshape(MLA_KV_RANK, MLA_WIDTH).astype(BF16)
    return w1, bf_row, wuq, wk, wv


def _row_tile(n):
    return n if n <= MAX_ROW_TILE else MAX_ROW_TILE


def _round_up(n, m):
    return -(-n // m) * m


def _group(x, pos0, past, mem_k, mem_v, wts):
    (gmix, w1, bf_row, gcq, wuq, gckv, wk, wv, gfox, gmla, wo, gmem, wmq, wmo, gffn, wg, wu, wd, gfin) = wts
    B, T, _ = x.shape
    tm = _row_tile(T)
    rope = _rope_tables(pos0 + jnp.arange(T, dtype=jnp.int32))
    qf, fk, fv, fvb, ckv, tail, mq = _proj(x, gmix, w1, bf_row, gcq, wuq, gckv, rope, tm)

    if past is None:
        q0, tq, tk = 0, min(ATTN_TQ, T), min(ATTN_TK, T)
        fq_s, fk_s = _foxprep(qf, fk, tail, tm)
        mk_s, mv_s = _mlakv(ckv, tail, wk, wv, tm)
        fvb_all, mq_s = fvb, mq
    else:
        c_fk, c_fv, c_lf, c_ckv, c_kr = past
        P = c_fk.shape[1]
        q0, tq, tk = P, _round_up(T, LANES), SAMPLE_TK
        S = _round_up(P + tq, tk)
        pad = lambda a: jnp.pad(a, ((0, 0), (0, S - a.shape[1]), (0, 0)))
        c_tail = jnp.concatenate([c_kr, c_lf, jnp.zeros((B, P, LANES - MLA_ROPE_DIM - FOX_HEADS), F32)], axis=2)
        tail_all = pad(jnp.concatenate([c_tail, tail], axis=1))
        k_all = pad(jnp.concatenate([c_fk.reshape(B, P, FOX_WIDTH), fk], axis=1))
        q_all = pad(jnp.concatenate([jnp.zeros((B, P, FOX_WIDTH), BF16), qf], axis=1))
        fvb_all = pad(jnp.concatenate([c_fv.reshape(B, P, FOX_WIDTH).astype(BF16), fvb], axis=1))
        ckv_all = pad(jnp.concatenate([c_ckv, ckv], axis=1))
        fq_all, fk_s = _foxprep(q_all, k_all, tail_all, tk)
        fq_s = fq_all[:, P:P + tq]
        mq_s = jnp.pad(mq, ((0, 0), (0, tq - T), (0, 0)))
        mk_s, mv_s = _mlakv(ckv_all, tail_all, wk, wv, tk)
    fox_o = _attention(fq_s, fk_s, _values_transposed(fvb_all, tk), tq=tq, tk=tk, q0=q0, unit=1,
                       name="fox_attn")[:, :T]
    mla_o = _attention(mq_s, mk_s, _values_transposed(mv_s, tk), tq=tq, tk=tk, q0=q0, unit=CHUNK,
                       name="mla_attn")[:, :T]
    y = _post(x, fox_o, mla_o, mem_k, mem_v, gfox, gmla, wo, gmem, wmq, wmo, gffn, wg, wu, wd, gfin, tm)
    return y, (fk, fv, tail, ckv)


def kernel(x_prompt, x_sample, mem_prompt, cache_fox_k, cache_fox_v, cache_fox_logf, cache_mla_ckv, cache_mla_kr, cache_mem_k, cache_mem_v, g_mix, w_in, b_f, g_cq, w_uq, g_ckv, w_ukv, g_fox_out, g_mla_out, w_o, g_mem_src, w_mk, w_mv, g_mem, w_mq, w_mo, g_ffn, w_gate, w_up, w_down, g_final):
    depth = w_in.shape[0]
    assert depth == 1, "one layer per step"
    l = 0
    Bp, Tp, _ = x_prompt.shape
    Bs, Ts, _ = x_sample.shape
    P = cache_fox_k.shape[2]
    row = lambda g: g.reshape(1, -1).astype(F32)
    w1, bf_row, wuq, wk, wv = _prep_weights(w_in[l], b_f[l], w_uq[l], w_ukv[l])
    wts = (row(g_mix[l]), w1, bf_row, row(g_cq[l]), wuq, row(g_ckv[l]), wk, wv,
           row(g_fox_out[l]), row(g_mla_out[l]), w_o[l].astype(BF16), row(g_mem[l]),
           w_mq[l].astype(BF16), w_mo[l].astype(BF16), row(g_ffn[l]),
           w_gate[l].astype(BF16), w_up[l].astype(BF16), w_down[l].astype(BF16), row(g_final))

    mk_p, mv_p = _memkv(mem_prompt, row(g_mem_src[l]), w_mk[l].astype(BF16), w_mv[l].astype(BF16))
    y_p, (fk_p, fv_p, tail_p, ckv_p) = _group(x_prompt, 0, None, mk_p, mv_p, wts)

    M = cache_mem_k.shape[2]
    past = (cache_fox_k[l], cache_fox_v[l], cache_fox_logf[l], cache_mla_ckv[l], cache_mla_kr[l])
    y_s, (fk_s, fv_s, tail_s, ckv_s) = _group(
        x_sample, P, past, cache_mem_k[l].reshape(Bs, M, MEM_WIDTH), cache_mem_v[l].reshape(Bs, M, MEM_WIDTH), wts)

    def rows(fk, fv, tail, ckv, B, T):
        return (fk.reshape(1, B, T, FOX_HEADS, FOX_HEAD_DIM), fv.reshape(1, B, T, FOX_HEADS, FOX_HEAD_DIM),
                tail[None, :, :, TAIL_F:TAIL_F + FOX_HEADS], ckv[None],
                tail[None, :, :, TAIL_KR:TAIL_KR + MLA_ROPE_DIM])

    rp = rows(fk_p, fv_p, tail_p, ckv_p, Bp, Tp)
    rs = rows(fk_s, fv_s, tail_s, ckv_s, Bs, Ts)
    mem_shape = (1, Bp, mem_prompt.shape[1], MEM_HEADS, MEM_HEAD_DIM)
    return (y_p, y_s) + rp + (mk_p.reshape(mem_shape), mv_p.reshape(mem_shape)) + rs
```
